```python
import jax
import jax.numpy as jnp
from jax import lax
import numpy as np

D_MODEL = 4096
BATCH = 2
SEQ = 8192
DEPTH = 4

GRID_W = 64
CTX_LEN = 256
N_MIXERS = 3
LAYER_KINDS = tuple(i % N_MIXERS for i in range(DEPTH))
N_NA_LAYERS = sum(1 for k in LAYER_KINDS if k == 0)
N_MLA_LAYERS = sum(1 for k in LAYER_KINDS if k == 1)
N_SW_LAYERS = sum(1 for k in LAYER_KINDS if k == 2)
RMS_EPS = 1e-6
ROPE_THETA = 10000.0
R_MOD = 512
Q_BLOCK = 128

NA_HEADS = 32
NA_HEAD_DIM = D_MODEL // NA_HEADS
NA_WIN_R = 8
NA_WIN_C = 16

MLA_HEADS = 32
MLA_Q_RANK = 1024
MLA_KV_RANK = 512
MLA_NOPE_DIM = 128
MLA_ROPE_DIM = 64
MLA_V_DIM = 128

SW_HEADS = 64
SW_KV_HEADS = 8
SW_HEAD_DIM = D_MODEL // SW_HEADS
SW_WINDOW = 128

N_EXPERTS = 16
N_GROUPS = 4
TOPK_GROUPS = 1
GROUP_SCORE_TOPK = 2
TOP_K = 2
D_EXPERT = 512

kernel_name = 'hybrid_diffusion_na_mla_swa_moe'

F32 = jnp.float32


def rmsnorm(x, g):
    xf = x.astype(F32)
    y = xf * lax.rsqrt(jnp.mean(xf * xf, axis=-1, keepdims=True) + RMS_EPS)
    return (y * g.astype(F32)).astype(x.dtype)


def adaln(cond, w_down, w_up, b):
    m = (jax.nn.silu(cond) @ w_down) @ w_up + b
    return [p[:, None, :] for p in jnp.split(m, 6, axis=-1)]


def modulate(h, shift, scale):
    return h * (1 + scale) + shift


def axial_rope_table(n_tokens, rot_dim, dtype):
    t = jnp.arange(n_tokens)
    row = (t // GRID_W).astype(F32)
    col = (t % GRID_W).astype(F32)
    d_ax = rot_dim // 2
    inv = ROPE_THETA ** (-jnp.arange(0, d_ax, 2, dtype=F32) / d_ax)
    ang_r = row[:, None] * inv
    ang_c = col[:, None] * inv
    return tuple(a[:, None, :].astype(dtype) for a in (jnp.cos(ang_r), jnp.sin(ang_r), jnp.cos(ang_c), jnp.sin(ang_c)))


def _rotate(x, cos, sin):
    x1, x2 = jnp.split(x, 2, axis=-1)
    return jnp.concatenate([x1 * cos - x2 * sin, x2 * cos + x1 * sin], axis=-1)


def apply_axial_rope(x, table):
    cr, sr, cc, sc = table
    xr, xc = jnp.split(x, 2, axis=-1)
    return jnp.concatenate([_rotate(xr, cr, sr), _rotate(xc, cc, sc)], axis=-1)


def neighbourhood_attention(h_ctx, h_lat, w_qkv, w_o, rpb, need_ctx):
    B, S, _ = h_lat.shape
    L = h_ctx.shape[1]
    H, dh = NA_HEADS, NA_HEAD_DIM
    rows = S // GRID_W
    win_r = min(NA_WIN_R, rows)
    n_loc = win_r * GRID_W
    scale = dh ** -0.5
    qkv = (jnp.concatenate([h_ctx, h_lat], axis=1) @ w_qkv).reshape(B, L + S, 3, H, dh)
    q, k, v = qkv[:, :, 0], qkv[:, :, 1], qkv[:, :, 2]
    q_c, k_c, v_c = q[:, :L], k[:, :L], v[:, :L]
    q_g = q[:, L:].reshape(B, rows, GRID_W, H, dh)
    k_g = k[:, L:].reshape(B, rows, GRID_W, H, dh)
    v_g = v[:, L:].reshape(B, rows, GRID_W, H, dh)
    qcol = jnp.arange(GRID_W)
    c0 = jnp.clip(qcol - NA_WIN_C // 2, 0, GRID_W - NA_WIN_C)
    kcol = jnp.arange(GRID_W)
    in_win = (kcol[None, :] >= c0[:, None]) & (kcol[None, :] < c0[:, None] + NA_WIN_C)
    dc_idx = jnp.clip(kcol[None, :] - qcol[:, None] + NA_WIN_C - 1, 0, 2 * NA_WIN_C - 2)
    bias = rpb.astype(F32)[:, :, dc_idx]
    bias = jnp.where(in_win, bias, -jnp.inf).transpose(0, 2, 1, 3)

    def row_block(r):
        r0 = jnp.clip(r - win_r // 2, 0, rows - win_r)
        q_r = lax.dynamic_index_in_dim(q_g, r, axis=1, keepdims=False)
        k_r = lax.dynamic_slice_in_dim(k_g, r0, win_r, axis=1).reshape(B, n_loc, H, dh)
        v_r = lax.dynamic_slice_in_dim(v_g, r0, win_r, axis=1).reshape(B, n_loc, H, dh)
        b_r = lax.dynamic_slice_in_dim(bias, r0 - r + NA_WIN_R - 1, win_r, axis=2).reshape(H, GRID_W, n_loc)
        s_loc = jnp.einsum('bqhd,bkhd->bhqk', q_r, k_r).astype(F32) * scale + b_r
        s_ctx = jnp.einsum('bqhd,bchd->bhqc', q_r, k_c).astype(F32) * scale
        p = jax.nn.softmax(jnp.concatenate([s_loc, s_ctx], axis=-1), axis=-1).astype(v_r.dtype)
        return (jnp.einsum('bhqk,bkhd->bqhd', p[..., :n_loc], v_r)
                + jnp.einsum('bhqc,bchd->bqhd', p[..., n_loc:], v_c))

    o = lax.map(row_block, jnp.arange(rows))
    o_lat = jnp.moveaxis(o, 0, 1).reshape(B, S, H * dh) @ w_o
    o_ctx = None
    if need_ctx:
        s = jnp.einsum('bqhd,bkhd->bhqk', q_c, k_c).astype(F32) * scale
        p = jax.nn.softmax(s, axis=-1).astype(v_c.dtype)
        o_ctx = jnp.einsum('bhqk,bkhd->bqhd', p, v_c).reshape(B, L, H * dh) @ w_o
    return o_lat, o_ctx


def latent_attention(h_ctx, h_lat, w_dq, g_q, w_uq, w_dkv, g_kv, w_ukv, w_o, rope, need_ctx):
    B, S, _ = h_lat.shape
    L = h_ctx.shape[1]
    T = L + S
    H = MLA_HEADS
    hcat = jnp.concatenate([h_ctx, h_lat], axis=1)
    q = (rmsnorm(hcat @ w_dq, g_q) @ w_uq).reshape(B, T, H, MLA_NOPE_DIM + MLA_ROPE_DIM)
    q_nope, q_pe = q[..., :MLA_NOPE_DIM], q[..., MLA_NOPE_DIM:]
    ckv = hcat @ w_dkv
    kv = (rmsnorm(ckv[..., :MLA_KV_RANK], g_kv) @ w_ukv).reshape(B, T, H, MLA_NOPE_DIM + MLA_V_DIM)
    k_nope, v = kv[..., :MLA_NOPE_DIM], kv[..., MLA_NOPE_DIM:]
    k_pe = ckv[..., MLA_KV_RANK:]
    q_pe = jnp.concatenate([q_pe[:, :L], apply_axial_rope(q_pe[:, L:], rope)], axis=1)
    k_pe = jnp.concatenate([k_pe[:, :L], apply_axial_rope(k_pe[:, L:, None, :], rope)[:, :, 0]], axis=1)
    scale = (MLA_NOPE_DIM + MLA_ROPE_DIM) ** -0.5

    def attend(qn, qp, kn, kp, vv):
        s = (jnp.einsum('bqhd,bkhd->bhqk', qn, kn).astype(F32)
             + jnp.einsum('bqhd,bkd->bhqk', qp, kp).astype(F32)) * scale
        p = jax.nn.softmax(s, axis=-1).astype(vv.dtype)
        return jnp.einsum('bhqk,bkhd->bqhd', p, vv)

    nb = S // Q_BLOCK
    qn_b = q_nope[:, L:].reshape(B, nb, Q_BLOCK, H, MLA_NOPE_DIM).swapaxes(0, 1)
    qp_b = q_pe[:, L:].reshape(B, nb, Q_BLOCK, H, MLA_ROPE_DIM).swapaxes(0, 1)
    o = lax.map(lambda qs: attend(qs[0], qs[1], k_nope, k_pe, v), (qn_b, qp_b))
    o_lat = o.swapaxes(0, 1).reshape(B, S, H * MLA_V_DIM) @ w_o
    o_ctx = None
    if need_ctx:
        o_c = attend(q_nope[:, :L], q_pe[:, :L], k_nope[:, :L], k_pe[:, :L], v[:, :L])
        o_ctx = o_c.reshape(B, L, H * MLA_V_DIM) @ w_o
    return o_lat, o_ctx


def sliding_window_attention(h_ctx, h_lat, w_q, w_kv, w_o, sink, rope, need_ctx):
    B, S, _ = h_lat.shape
    L = h_ctx.shape[1]
    T = L + S
    H, KVH, dh = SW_HEADS, SW_KV_HEADS, SW_HEAD_DIM
    G = H // KVH
    hcat = jnp.concatenate([h_ctx, h_lat], axis=1)
    q = (hcat @ w_q).reshape(B, T, H, dh)
    kv = (hcat @ w_kv).reshape(B, T, 2, KVH, dh)
    k, v = kv[:, :, 0], kv[:, :, 1]
    q = jnp.concatenate([q[:, :L], apply_axial_rope(q[:, L:], rope)], axis=1).reshape(B, T, KVH, G, dh)
    k = jnp.concatenate([k[:, :L], apply_axial_rope(k[:, L:], rope)], axis=1)
    scale = dh ** -0.5
    sink_f = sink.astype(F32).reshape(KVH, G)[None, :, :, None, None]
    k_c, v_c = k[:, :L], v[:, :L]
    band = Q_BLOCK + 2 * SW_WINDOW
    pad = ((0, 0), (SW_WINDOW, SW_WINDOW), (0, 0), (0, 0))
    k_p = jnp.pad(k[:, L:], pad)
    v_p = jnp.pad(v[:, L:], pad)
    qi = jnp.arange(Q_BLOCK)[:, None]
    kj = jnp.arange(band)[None, :]
    in_band = jnp.abs(kj - SW_WINDOW - qi) <= SW_WINDOW
    nb = S // Q_BLOCK
    q_b = q[:, L:].reshape(B, nb, Q_BLOCK, KVH, G, dh).swapaxes(0, 1)

    def block(args):
        bi, qb = args
        start = bi * Q_BLOCK
        kb = lax.dynamic_slice_in_dim(k_p, start, band, axis=1)
        vb = lax.dynamic_slice_in_dim(v_p, start, band, axis=1)
        kpos = start - SW_WINDOW + kj
        valid = in_band & (kpos >= 0) & (kpos < S)
        s_win = jnp.where(valid, jnp.einsum('bqkgd,bnkd->bkgqn', qb, kb).astype(F32) * scale, -jnp.inf)
        s_ctx = jnp.einsum('bqkgd,bckd->bkgqc', qb, k_c).astype(F32) * scale
        s_sink = jnp.broadcast_to(sink_f, s_ctx.shape[:-1] + (1,))
        p = jax.nn.softmax(jnp.concatenate([s_win, s_ctx, s_sink], axis=-1), axis=-1)[..., :-1].astype(vb.dtype)
        return (jnp.einsum('bkgqn,bnkd->bqkgd', p[..., :band], vb)
                + jnp.einsum('bkgqc,bckd->bqkgd', p[..., band:], v_c))

    o = lax.map(block, (jnp.arange(nb), q_b))
    o_lat = o.swapaxes(0, 1).reshape(B, S, H * dh) @ w_o
    o_ctx = None
    if need_ctx:
        q_c = q[:, :L]
        s = jnp.einsum('bqkgd,bckd->bkgqc', q_c, k_c).astype(F32) * scale
        s = jnp.concatenate([s, jnp.broadcast_to(sink_f, s.shape[:-1] + (1,))], axis=-1)
        p = jax.nn.softmax(s, axis=-1)[..., :-1].astype(v_c.dtype)
        o_ctx = jnp.einsum('bkgqc,bckd->bqkgd', p, v_c).reshape(B, L, H * dh) @ w_o
    return o_lat, o_ctx


def route(x2d, w_r, b_r):
    s = jax.nn.sigmoid((x2d @ w_r).astype(F32))
    s_sel = s + b_r.astype(F32)
    grp = s_sel.reshape(-1, N_GROUPS, N_EXPERTS // N_GROUPS)
    grp_score = lax.top_k(grp, GROUP_SCORE_TOPK)[0].sum(-1)
    _, g_idx = lax.top_k(grp_score, TOPK_GROUPS)
    keep = jnp.sum(jax.nn.one_hot(g_idx, N_GROUPS, dtype=F32), axis=1) > 0
    masked = jnp.where(keep[:, :, None], grp, -jnp.inf).reshape(-1, N_EXPERTS)
    _, idx = lax.top_k(masked, TOP_K)
    w = jnp.take_along_axis(s, idx, axis=-1)
    w = w / jnp.sum(w, axis=-1, keepdims=True)
    return jnp.sum(jax.nn.one_hot(idx, N_EXPERTS, dtype=F32) * w[..., None], axis=1)


def moe(x2d, w_r, b_r, w_gate, w_up, w_down):
    gates = route(x2d, w_r, b_r).astype(x2d.dtype)
    y = jnp.zeros_like(x2d)
    for e in range(N_EXPERTS):
        hdn = jax.nn.silu(x2d @ w_gate[e]) * (x2d @ w_up[e])
        y = y + gates[:, e:e + 1] * (hdn @ w_down[e])
    return y


def setup_inputs(seed: int = 0) -> dict:
    key = jax.random.key(seed)
    ks = iter(jax.random.split(key, 40))
    D = D_MODEL

    def nrm(shape, std):
        return jax.random.normal(next(ks), shape, F32) * std

    def gain(shape):
        return 1.0 + nrm(shape, 0.02)

    return {
        'x': nrm((BATCH, SEQ, D), 1.0),
        'c': nrm((BATCH, D), 1.0),
        'ctx': nrm((BATCH, CTX_LEN, D), 1.0),
        'c_ctx': nrm((D,), 1.0),
        'norm1_g': gain((DEPTH, D)),
        'norm2_g': gain((DEPTH, D)),
        'final_g': gain((D,)),
        'mod_down': nrm((DEPTH, D, R_MOD), D ** -0.5),
        'mod_up': nrm((DEPTH, R_MOD, 6 * D), 0.5 * R_MOD ** -0.5),
        'mod_b': nrm((DEPTH, 6 * D), 0.02),
        'na_wqkv': nrm((N_NA_LAYERS, D, 3 * NA_HEADS * NA_HEAD_DIM), D ** -0.5),
        'na_wo': nrm((N_NA_LAYERS, NA_HEADS * NA_HEAD_DIM, D), (NA_HEADS * NA_HEAD_DIM) ** -0.5),
        'na_rpb': nrm((N_NA_LAYERS, NA_HEADS, 2 * NA_WIN_R - 1, 2 * NA_WIN_C - 1), 0.1),
        'mla_wdq': nrm((N_MLA_LAYERS, D, MLA_Q_RANK), D ** -0.5),
        'mla_qnorm': gain((N_MLA_LAYERS, MLA_Q_RANK)),
        'mla_wuq': nrm((N_MLA_LAYERS, MLA_Q_RANK, MLA_HEADS * (MLA_NOPE_DIM + MLA_ROPE_DIM)), MLA_Q_RANK ** -0.5),
        'mla_wdkv': nrm((N_MLA_LAYERS, D, MLA_KV_RANK + MLA_ROPE_DIM), D ** -0.5),
        'mla_kvnorm': gain((N_MLA_LAYERS, MLA_KV_RANK)),
        'mla_wukv': nrm((N_MLA_LAYERS, MLA_KV_RANK, MLA_HEADS * (MLA_NOPE_DIM + MLA_V_DIM)), MLA_KV_RANK ** -0.5),
        'mla_wo': nrm((N_MLA_LAYERS, MLA_HEADS * MLA_V_DIM, D), (MLA_HEADS * MLA_V_DIM) ** -0.5),
        'sw_wq': nrm((N_SW_LAYERS, D, SW_HEADS * SW_HEAD_DIM), D ** -0.5),
        'sw_wkv': nrm((N_SW_LAYERS, D, 2 * SW_KV_HEADS * SW_HEAD_DIM), D ** -0.5),
        'sw_wo': nrm((N_SW_LAYERS, SW_HEADS * SW_HEAD_DIM, D), (SW_HEADS * SW_HEAD_DIM) ** -0.5),
        'sw_sink': nrm((N_SW_LAYERS, SW_HEADS), 1.0),
        'router_w': nrm((D, N_EXPERTS), D ** -0.5),
        'router_b': nrm((N_EXPERTS,), 0.01),
        'moe_wg': nrm((DEPTH, N_EXPERTS, D, D_EXPERT), D ** -0.5),
        'moe_wu': nrm((DEPTH, N_EXPERTS, D, D_EXPERT), D ** -0.5),
        'moe_wd': nrm((DEPTH, N_EXPERTS, D_EXPERT, D), D_EXPERT ** -0.5),
    }


def reference(x, c, ctx, c_ctx, norm1_g, norm2_g, final_g, mod_down, mod_up, mod_b,
              na_wqkv, na_wo, na_rpb, mla_wdq, mla_qnorm, mla_wuq, mla_wdkv, mla_kvnorm, mla_wukv, mla_wo,
              sw_wq, sw_wkv, sw_wo, sw_sink, router_w, router_b, moe_wg, moe_wu, moe_wd):
    B, S, D = x.shape
    L = ctx.shape[1]
    rope_mla = axial_rope_table(S, MLA_ROPE_DIM, x.dtype)
    rope_sw = axial_rope_table(S, SW_HEAD_DIM, x.dtype)
    xl, xc = x, ctx
    na_i = mla_i = sw_i = 0
    for layer in range(DEPTH):
        need_ctx = layer < DEPTH - 1
        ml = adaln(c, mod_down[layer], mod_up[layer], mod_b[layer])
        mc = adaln(c_ctx[None, :], mod_down[layer], mod_up[layer], mod_b[layer])
        hl = modulate(rmsnorm(xl, norm1_g[layer]), ml[0], ml[1])
        hc = modulate(rmsnorm(xc, norm1_g[layer]), mc[0], mc[1])
        kind = LAYER_KINDS[layer]
        if kind == 0:
            ol, oc = neighbourhood_attention(hc, hl, na_wqkv[na_i], na_wo[na_i], na_rpb[na_i], need_ctx)
            na_i += 1
        elif kind == 1:
            ol, oc = latent_attention(hc, hl, mla_wdq[mla_i], mla_qnorm[mla_i], mla_wuq[mla_i], mla_wdkv[mla_i],
                                      mla_kvnorm[mla_i], mla_wukv[mla_i], mla_wo[mla_i], rope_mla, need_ctx)
            mla_i += 1
        else:
            ol, oc = sliding_window_attention(hc, hl, sw_wq[sw_i], sw_wkv[sw_i], sw_wo[sw_i], sw_sink[sw_i],
                                              rope_sw, need_ctx)
            sw_i += 1
        xl = xl + ml[2] * ol
        hl = modulate(rmsnorm(xl, norm2_g[layer]), ml[3], ml[4])
        if need_ctx:
            xc = xc + mc[2] * oc
            hc = modulate(rmsnorm(xc, norm2_g[layer]), mc[3], mc[4])
            tokens = jnp.concatenate([hc, hl], axis=1).reshape(-1, D)
            y = moe(tokens, router_w, router_b, moe_wg[layer], moe_wu[layer], moe_wd[layer]).reshape(B, L + S, D)
            xc = xc + mc[5] * y[:, :L]
            xl = xl + ml[5] * y[:, L:]
        else:
            y = moe(hl.reshape(-1, D), router_w, router_b, moe_wg[layer], moe_wu[layer], moe_wd[layer])
            xl = xl + ml[5] * y.reshape(B, S, D)
    return rmsnorm(xl, final_g)
```

```python
import functools

import numpy as np
import jax
import jax.numpy as jnp
from jax import lax
from jax.experimental import pallas as pl
from jax.experimental.pallas import tpu as pltpu

F32 = jnp.float32
BF16 = jnp.bfloat16

GRID_W = 64
N_MIXERS = 3
RMS_EPS = 1e-6
ROPE_THETA = 10000.0

NA_HEADS = 32
NA_WIN_R = 8
NA_WIN_C = 16
NA_ROW_BLOCK = 4
NA_UNION = NA_WIN_R + NA_ROW_BLOCK - 1

MLA_HEADS = 32
MLA_NOPE_DIM = 128
MLA_ROPE_DIM = 64
MLA_V_DIM = 128
MLA_TQ = 512
MLA_TK = 512

SW_HEADS = 64
SW_KV_HEADS = 8
SW_WINDOW = 128
SW_QB = 128

N_EXPERTS = 16
N_GROUPS = 4
EXPERTS_PER_GROUP = N_EXPERTS // N_GROUPS

LANE = 128
TM = 512
TR = 256
VMEM_LIMIT = 52 * 1024 * 1024

NEG_INF = float("-inf")


def _cparams(n_axes, vmem=VMEM_LIMIT):
    return pltpu.CompilerParams(dimension_semantics=("arbitrary",) * n_axes, vmem_limit_bytes=vmem)


def _mod_index(i, tile, n_lat, seq):
    row = i * tile
    return jnp.where(row < n_lat, 1 + row // seq, 0)


def _adaln_down_kernel(c_ref, w_ref, o_ref):
    c = c_ref[...]
    s = c * jax.nn.sigmoid(c)
    o_ref[...] = jnp.dot(s, w_ref[...], preferred_element_type=F32, precision=lax.Precision.HIGHEST)


def _adaln_up_kernel(t_ref, w_ref, b_ref, o_ref):
    o_ref[...] = jnp.dot(t_ref[...], w_ref[...], preferred_element_type=F32,
                         precision=lax.Precision.HIGHEST) + b_ref[...]


def _adaln_all(cond8, mod_down, mod_up, mod_b):
    depth, d, r = mod_down.shape
    m6 = mod_up.shape[2]
    t = pl.pallas_call(
        _adaln_down_kernel,
        grid=(depth,),
        in_specs=[pl.BlockSpec((8, d), lambda l: (0, 0)),
                  pl.BlockSpec((None, d, r), lambda l: (l, 0, 0))],
        out_specs=pl.BlockSpec((None, 8, r), lambda l: (l, 0, 0)),
        out_shape=jax.ShapeDtypeStruct((depth, 8, r), F32),
        compiler_params=_cparams(1),
        name="adaln_down",
    )(cond8, mod_down)
    tn = min(m6, 2048)
    return pl.pallas_call(
        _adaln_up_kernel,
        grid=(depth, m6 // tn),
        in_specs=[pl.BlockSpec((None, 8, r), lambda l, j: (l, 0, 0)),
                  pl.BlockSpec((None, r, tn), lambda l, j: (l, 0, j)),
                  pl.BlockSpec((None, 1, tn), lambda l, j: (l, 0, j))],
        out_specs=pl.BlockSpec((None, 8, tn), lambda l, j: (l, 0, j)),
        out_shape=jax.ShapeDtypeStruct((depth, 8, m6), F32),
        compiler_params=_cparams(2),
        name="adaln_up",
    )(t, mod_up, mod_b.reshape(depth, 1, m6))


def _rms(x, g):
    return x * lax.rsqrt(jnp.mean(x * x, axis=-1, keepdims=True) + RMS_EPS) * g


def _norm_mod_kernel(x_ref, g_ref, sh_ref, sc_ref, o_ref):
    y = _rms(x_ref[...], g_ref[...])
    o_ref[...] = (y * (1.0 + sc_ref[...]) + sh_ref[...]).astype(o_ref.dtype)


def _norm_mod(x, g, mods, layer, which, n_rows, n_lat, seq, out_dtype=BF16):
    d = x.shape[1]
    midx = functools.partial(_mod_index, tile=TR, n_lat=n_lat, seq=seq)
    return pl.pallas_call(
        _norm_mod_kernel,
        grid=(n_rows // TR,),
        in_specs=[pl.BlockSpec((TR, d), lambda i: (i, 0)),
                  pl.BlockSpec((None, 1, d), lambda i: (layer, 0, 0)),
                  pl.BlockSpec((None, None, None, 1, d), lambda i: (layer, midx(i), which, 0, 0)),
                  pl.BlockSpec((None, None, None, 1, d), lambda i: (layer, midx(i), which + 1, 0, 0))],
        out_specs=pl.BlockSpec((TR, d), lambda i: (i, 0)),
        out_shape=jax.ShapeDtypeStruct((n_rows, d), out_dtype),
        compiler_params=_cparams(1),
        name="norm_mod",
    )(x, g, mods, mods)


def _final_norm_kernel(x_ref, g_ref, o_ref):
    o_ref[...] = _rms(x_ref[...], g_ref[...])


def _final_norm(x, g, n_rows):
    d = x.shape[1]
    return pl.pallas_call(
        _final_norm_kernel,
        grid=(n_rows // TR,),
        in_specs=[pl.BlockSpec((TR, d), lambda i: (i, 0)),
                  pl.BlockSpec((1, d), lambda i: (0, 0))],
        out_specs=pl.BlockSpec((TR, d), lambda i: (i, 0)),
        out_shape=jax.ShapeDtypeStruct((n_rows, d), F32),
        compiler_params=_cparams(1),
        name="final_norm",
    )(x, g)


def _first_min_lane(mask_val, lane_f):
    return jnp.min(jnp.where(mask_val, lane_f, 1e9), axis=-1, keepdims=True)


def _norm_route_kernel(x_ref, g_ref, sh_ref, sc_ref, wr_ref, rb_ref, h_ref, gt_ref):
    y = _rms(x_ref[...], g_ref[...])
    h = y * (1.0 + sc_ref[...]) + sh_ref[...]
    h_ref[...] = h
    hh = h.astype(BF16)
    hl = (h - hh.astype(F32)).astype(BF16)
    w2 = wr_ref[...]
    t = (jnp.dot(hh, w2, preferred_element_type=F32) + jnp.dot(hl, w2, preferred_element_type=F32))
    logits = t + pltpu.roll(t, LANE - N_EXPERTS, 1)
    rows = logits.shape[0]
    lane = lax.broadcasted_iota(jnp.int32, (rows, LANE), 1)
    lane_f = lane.astype(F32)
    valid = lane < N_EXPERTS
    grp = lane >> 2
    s = jax.nn.sigmoid(logits)
    s_sel = s + rb_ref[...]

    def top2(masked):
        m1 = jnp.max(masked, axis=-1, keepdims=True)
        i1 = _first_min_lane(masked == m1, lane_f)
        rest = jnp.where(lane_f == i1, NEG_INF, masked)
        m2 = jnp.max(rest, axis=-1, keepdims=True)
        i2 = _first_min_lane(rest == m2, lane_f)
        return m1, i1, m2, i2

    scores = []
    for gi in range(N_GROUPS):
        m1, _, m2, _ = top2(jnp.where((grp == gi) & valid, s_sel, NEG_INF))
        scores.append(m1 + m2)
    best = functools.reduce(jnp.maximum, scores)
    gsel = jnp.full_like(best, float(N_GROUPS - 1))
    for gi in range(N_GROUPS - 2, -1, -1):
        gsel = jnp.where(scores[gi] == best, float(gi), gsel)
    in_grp = (grp.astype(F32) == gsel) & valid
    _, i1, _, i2 = top2(jnp.where(in_grp, s_sel, NEG_INF))
    w1 = jnp.sum(jnp.where(lane_f == i1, s, 0.0), axis=-1, keepdims=True)
    w2s = jnp.sum(jnp.where(lane_f == i2, s, 0.0), axis=-1, keepdims=True)
    den = w1 + w2s
    gates = jnp.where(lane_f == i1, w1 / den, 0.0) + jnp.where(lane_f == i2, w2s / den, 0.0)
    gt_ref[...] = jnp.where(lane == N_EXPERTS, gsel, gates)


def _norm_route(x, g, mods, layer, wr2, rb, n_rows, n_lat, seq):
    d = x.shape[1]
    midx = functools.partial(_mod_index, tile=TR, n_lat=n_lat, seq=seq)
    return pl.pallas_call(
        _norm_route_kernel,
        grid=(n_rows // TR,),
        in_specs=[pl.BlockSpec((TR, d), lambda i: (i, 0)),
                  pl.BlockSpec((None, 1, d), lambda i: (layer, 0, 0)),
                  pl.BlockSpec((None, None, None, 1, d), lambda i: (layer, midx(i), 3, 0, 0)),
                  pl.BlockSpec((None, None, None, 1, d), lambda i: (layer, midx(i), 4, 0, 0)),
                  pl.BlockSpec((d, LANE), lambda i: (0, 0)),
                  pl.BlockSpec((1, LANE), lambda i: (0, 0))],
        out_specs=[pl.BlockSpec((TR, d), lambda i: (i, 0)),
                   pl.BlockSpec((TR, LANE), lambda i: (i, 0))],
        out_shape=[jax.ShapeDtypeStruct((n_rows, d), F32),
                   jax.ShapeDtypeStruct((n_rows, LANE), F32)],
        compiler_params=_cparams(1),
        name="norm_route",
    )(x, g, mods, mods, wr2, rb)


def _rope128(x, cos, sin):
    lane = lax.broadcasted_iota(jnp.int32, x.shape, 1)
    first = (lane & 31) < 16
    partner = jnp.where(first, pltpu.roll(x, LANE - 16, 1), pltpu.roll(x, 16, 1))
    return x * cos + partner * sin


def _mm_kernel(*refs, nk, epi, rope_mask, rms_width):
    a_ref, w_ref = refs[0], refs[1]
    acc_ref = refs[-1] if nk > 1 else None
    rest = refs[2:-1] if nk > 1 else refs[2:]
    k = pl.program_id(2)
    part = jnp.dot(a_ref[...], w_ref[...].astype(BF16), preferred_element_type=F32)

    def finish(acc):
        if epi == "plain":
            (o_ref,) = rest
            o_ref[...] = acc.astype(o_ref.dtype)
        elif epi == "rope":
            cos_ref, sin_ref, o_ref = rest
            cos, sin = cos_ref[...], sin_ref[...]
            for gi, on in enumerate(rope_mask):
                blk = acc[:, gi * LANE:(gi + 1) * LANE]
                if on:
                    blk = _rope128(blk, cos, sin)
                o_ref[:, gi * LANE:(gi + 1) * LANE] = blk.astype(o_ref.dtype)
        elif epi == "rms":
            if rms_width < acc.shape[1]:
                g_ref, cos_ref, sin_ref, o_ref, o2_ref = rest
                cos, sin = cos_ref[...], sin_ref[...]
                for gi in range((acc.shape[1] - rms_width) // LANE):
                    blk = acc[:, rms_width + gi * LANE: rms_width + (gi + 1) * LANE]
                    o2_ref[:, gi * LANE:(gi + 1) * LANE] = _rope128(blk, cos, sin).astype(o2_ref.dtype)
            else:
                g_ref, o_ref = rest
            o_ref[...] = _rms(acc[:, :rms_width], g_ref[...]).astype(o_ref.dtype)
        elif epi == "resid":
            res_ref, gate_ref, o_ref = rest
            o_ref[...] = res_ref[...] + gate_ref[...] * acc

    if nk == 1:
        finish(part)
    else:
        @pl.when(k == 0)
        def _():
            acc_ref[...] = part

        @pl.when(k > 0)
        def _():
            acc_ref[...] += part

        @pl.when(k == nk - 1)
        def _():
            finish(acc_ref[...])


def _mm(a, w, *, layer=None, n_rows, tn, tk, epi="plain", out_dtype=BF16, extra=(), extra_specs=(),
        rope_mask=(), rms_width=0, out_widths=None):
    kdim = a.shape[1]
    m = w.shape[-1]
    nk = kdim // tk
    grid = (m // tn, n_rows // TM, nk)
    if layer is None:
        w_spec = pl.BlockSpec((tk, tn), lambda j, i, k: (k, j))
    else:
        w_spec = pl.BlockSpec((None, tk, tn), lambda j, i, k: (layer, k, j))
    in_specs = [pl.BlockSpec((TM, tk), lambda j, i, k: (i, k)), w_spec, *extra_specs]
    if out_widths is None:
        out_specs = pl.BlockSpec((TM, tn), lambda j, i, k: (i, j))
        out_shape = jax.ShapeDtypeStruct((n_rows, m), out_dtype)
    else:
        out_specs = [pl.BlockSpec((TM, wd), lambda j, i, k: (i, 0)) for wd in out_widths]
        out_shape = [jax.ShapeDtypeStruct((n_rows, wd), out_dtype) for wd in out_widths]
    scratch = [pltpu.VMEM((TM, tn), F32)] if nk > 1 else []
    return pl.pallas_call(
        functools.partial(_mm_kernel, nk=nk, epi=epi, rope_mask=tuple(rope_mask), rms_width=rms_width),
        grid=grid, in_specs=in_specs, out_specs=out_specs, out_shape=out_shape,
        scratch_shapes=scratch, compiler_params=_cparams(3), name="mm_" + epi,
    )(a, w, *extra)


def _rope_specs():
    return [pl.BlockSpec((TM, LANE), lambda j, i, k: (i, 0)), pl.BlockSpec((TM, LANE), lambda j, i, k: (i, 0))]


def _mm_resid(a, w, layer, res, mods, mod_layer, which, n_rows, n_lat, seq, tn=512):
    midx = functools.partial(_mod_index, tile=TM, n_lat=n_lat, seq=seq)
    specs = [pl.BlockSpec((TM, tn), lambda j, i, k: (i, j)),
             pl.BlockSpec((None, None, None, 1, tn), lambda j, i, k: (mod_layer, midx(i), which, 0, j))]
    return _mm(a, w, layer=layer, n_rows=n_rows, tn=tn, tk=a.shape[1], epi="resid", out_dtype=F32,
               extra=(res, mods), extra_specs=specs)


def _na_tables(rows):
    nm = rows // NA_ROW_BLOCK
    patterns, u0s, sels = [], [], []
    for m in range(nm):
        rf = NA_ROW_BLOCK * m
        u0 = int(np.clip(rf - NA_WIN_R // 2, 0, rows - NA_UNION))
        drmap = np.full((NA_ROW_BLOCK, NA_UNION), 2 * NA_WIN_R - 1, np.int32)
        for i in range(NA_ROW_BLOCK):
            r = rf + i
            r0 = int(np.clip(r - NA_WIN_R // 2, 0, rows - NA_WIN_R))
            for j in range(NA_UNION):
                kr = u0 + j
                if r0 <= kr < r0 + NA_WIN_R:
                    drmap[i, j] = kr - r + NA_WIN_R - 1
        key = drmap.tobytes()
        if key not in [p.tobytes() for p in patterns]:
            patterns.append(drmap)
        sels.append([p.tobytes() for p in patterns].index(key))
        u0s.append(u0)
    return np.stack(patterns), np.array([u0s, sels], np.int32)


def _na_bias_slabs(rpb, patterns):
    h = rpb.shape[0]
    qc = np.arange(GRID_W)
    kc = np.arange(GRID_W)
    c0 = np.clip(qc - NA_WIN_C // 2, 0, GRID_W - NA_WIN_C)
    in_c = (kc[None, :] >= c0[:, None]) & (kc[None, :] < c0[:, None] + NA_WIN_C)
    dc = np.clip(kc[None, :] - qc[:, None] + NA_WIN_C - 1, 0, 2 * NA_WIN_C - 2)
    onehot = (dc.reshape(-1)[None, :] == np.arange(2 * NA_WIN_C - 1)[:, None]).astype(np.float32)
    rc = jnp.einsum("hrd,dk->hrk", rpb.astype(F32), jnp.asarray(onehot), precision=lax.Precision.HIGHEST)
    rc = rc.reshape(h, 2 * NA_WIN_R - 1, GRID_W, GRID_W)
    rc = jnp.where(jnp.asarray(in_c), rc, NEG_INF)
    rc = jnp.concatenate([rc, jnp.full((h, 1, GRID_W, GRID_W), NEG_INF, F32)], axis=1)
    ns = patterns.shape[0]
    slab = jnp.take(rc, jnp.asarray(patterns.reshape(-1)), axis=1)
    slab = slab.reshape(h, ns, NA_ROW_BLOCK, NA_UNION, GRID_W, GRID_W).transpose(0, 1, 2, 4, 3, 5)
    return slab.reshape(h, ns, NA_ROW_BLOCK * GRID_W, NA_UNION * GRID_W)


def _softmax_pv(parts, extra_logit=None):
    m = functools.reduce(jnp.maximum, [jnp.max(s, axis=-1, keepdims=True) for s, _ in parts])
    if extra_logit is not None:
        m = jnp.maximum(m, extra_logit)
    l = 0.0 if extra_logit is None else jnp.exp(extra_logit - m)
    o = 0.0
    for s, v in parts:
        p = jnp.exp(s - m)
        l = l + jnp.sum(p, axis=-1, keepdims=True)
        o = o + jnp.dot(p.astype(BF16), v, preferred_element_type=F32)
    return o / l


def _qk(q, k):
    return lax.dot_general(q, k, (((1,), (1,)), ((), ())), preferred_element_type=F32)


def _na_kernel(tab_ref, q_ref, k_ref, v_ref, kc_ref, vc_ref, b_ref, o_ref, *, scale):
    m = pl.program_id(2)
    start = pl.multiple_of(tab_ref[0, m] * GRID_W, GRID_W)
    nkeys = NA_UNION * GRID_W
    q = q_ref[...]
    s_loc = _qk(q, k_ref[pl.ds(start, nkeys), :]) * scale + b_ref[tab_ref[1, m]]
    s_ctx = _qk(q, kc_ref[...]) * scale
    o = _softmax_pv([(s_loc, v_ref[pl.ds(start, nkeys), :]), (s_ctx, vc_ref[...])])
    o_ref[...] = o.astype(o_ref.dtype)


def _ctx_attn_kernel(q_ref, k_ref, v_ref, o_ref, *, scale):
    o = _softmax_pv([(_qk(q_ref[...], k_ref[...]) * scale, v_ref[...])])
    o_ref[...] = o.astype(o_ref.dtype)


def _na_attention(qkv, slabs, tab, batch, seq, ctx_len, need_ctx):
    n = qkv.shape[0]
    hd = qkv.shape[1] // 3
    heads = NA_HEADS
    dh = hd // heads
    scale = dh ** -0.5
    qb = NA_ROW_BLOCK * GRID_W
    nm = seq // qb
    cb = (batch * seq) // ctx_len
    ns = slabs.shape[1]
    grid_spec = pltpu.PrefetchScalarGridSpec(
        num_scalar_prefetch=1, grid=(batch, heads, nm),
        in_specs=[pl.BlockSpec((qb, dh), lambda b, h, m, t: (b * nm + m, h)),
                  pl.BlockSpec((seq, dh), lambda b, h, m, t: (b, heads + h)),
                  pl.BlockSpec((seq, dh), lambda b, h, m, t: (b, 2 * heads + h)),
                  pl.BlockSpec((ctx_len, dh), lambda b, h, m, t: (cb + b, heads + h)),
                  pl.BlockSpec((ctx_len, dh), lambda b, h, m, t: (cb + b, 2 * heads + h)),
                  pl.BlockSpec((None, ns, qb, NA_UNION * GRID_W), lambda b, h, m, t: (h, 0, 0, 0))],
        out_specs=pl.BlockSpec((qb, dh), lambda b, h, m, t: (b * nm + m, h)))
    out = pl.pallas_call(
        functools.partial(_na_kernel, scale=scale), grid_spec=grid_spec,
        out_shape=jax.ShapeDtypeStruct((batch * seq, hd), BF16), compiler_params=_cparams(3), name="na_attn",
    )(tab, qkv, qkv, qkv, qkv, qkv, slabs)
    if not need_ctx:
        return out
    out_c = pl.pallas_call(
        functools.partial(_ctx_attn_kernel, scale=scale), grid=(batch, heads),
        in_specs=[pl.BlockSpec((ctx_len, dh), lambda b, h: (cb + b, h)),
                  pl.BlockSpec((ctx_len, dh), lambda b, h: (cb + b, heads + h)),
                  pl.BlockSpec((ctx_len, dh), lambda b, h: (cb + b, 2 * heads + h))],
        out_specs=pl.BlockSpec((ctx_len, dh), lambda b, h: (b, h)),
        out_shape=jax.ShapeDtypeStruct((batch * ctx_len, hd), BF16), compiler_params=_cparams(2),
        name="na_ctx_attn",
    )(qkv, qkv, qkv)
    return jnp.concatenate([out, out_c], axis=0)


def _mla_kernel(*refs, scale, tk, n_chunks, latent):
    if latent:
        q_ref, kn_ref, kp_ref, v_ref, knc_ref, kpc_ref, vc_ref, o_ref = refs
    else:
        q_ref, knc_ref, kpc_ref, vc_ref, o_ref = refs
    q = q_ref[...]
    tq = q.shape[0]

    def step(kn, kp, vv, m, l, acc):
        s = _qk(q, jnp.concatenate([kn, kp], axis=1)) * scale
        m_new = jnp.maximum(m, jnp.max(s, axis=-1, keepdims=True))
        alpha = jnp.exp(m - m_new)
        p = jnp.exp(s - m_new)
        l = alpha * l + jnp.sum(p, axis=-1, keepdims=True)
        acc = alpha * acc + jnp.dot(p.astype(BF16), vv, preferred_element_type=F32)
        return m_new, l, acc

    carry = (jnp.full((tq, 1), NEG_INF, F32), jnp.zeros((tq, 1), F32), jnp.zeros((tq, vc_ref.shape[1]), F32))
    carry = step(knc_ref[...], kpc_ref[...], vc_ref[...], *carry)
    if latent:
        def body(c, carry):
            st = pl.multiple_of(c * tk, tk)
            return step(kn_ref[pl.ds(st, tk), :], kp_ref[pl.ds(st, tk), :], v_ref[pl.ds(st, tk), :], *carry)
        carry = lax.fori_loop(0, n_chunks, body, carry)
    _, l, acc = carry
    o_ref[...] = (acc / l).astype(o_ref.dtype)


def _mla_attention(q, kv, kpe, batch, seq, ctx_len, need_ctx):
    n = q.shape[0]
    heads = MLA_HEADS
    dq = q.shape[1] // heads
    dn, dv = MLA_NOPE_DIM, MLA_V_DIM
    scale = (MLA_NOPE_DIM + MLA_ROPE_DIM) ** -0.5
    nq = seq // MLA_TQ
    cb = (batch * seq) // ctx_len
    vo = heads * dn // dv
    out = pl.pallas_call(
        functools.partial(_mla_kernel, scale=scale, tk=MLA_TK, n_chunks=seq // MLA_TK, latent=True),
        grid=(batch, heads, nq),
        in_specs=[pl.BlockSpec((MLA_TQ, dq), lambda b, h, i: (b * nq + i, h)),
                  pl.BlockSpec((seq, dn), lambda b, h, i: (b, h)),
                  pl.BlockSpec((seq, LANE), lambda b, h, i: (b, 0)),
                  pl.BlockSpec((seq, dv), lambda b, h, i: (b, vo + h)),
                  pl.BlockSpec((ctx_len, dn), lambda b, h, i: (cb + b, h)),
                  pl.BlockSpec((ctx_len, LANE), lambda b, h, i: (cb + b, 0)),
                  pl.BlockSpec((ctx_len, dv), lambda b, h, i: (cb + b, vo + h))],
        out_specs=pl.BlockSpec((MLA_TQ, dv), lambda b, h, i: (b * nq + i, h)),
        out_shape=jax.ShapeDtypeStruct((batch * seq, heads * dv), BF16), compiler_params=_cparams(3), name="mla_attn",
    )(q, kv, kpe, kv, kv, kpe, kv)
    if not need_ctx:
        return out
    out_c = pl.pallas_call(
        functools.partial(_mla_kernel, scale=scale, tk=MLA_TK, n_chunks=0, latent=False),
        grid=(batch, heads),
        in_specs=[pl.BlockSpec((ctx_len, dq), lambda b, h: (cb + b, h)),
                  pl.BlockSpec((ctx_len, dn), lambda b, h: (cb + b, h)),
                  pl.BlockSpec((ctx_len, LANE), lambda b, h: (cb + b, 0)),
                  pl.BlockSpec((ctx_len, dv), lambda b, h: (cb + b, vo + h))],
        out_specs=pl.BlockSpec((ctx_len, dv), lambda b, h: (b, h)),
        out_shape=jax.ShapeDtypeStruct((batch * ctx_len, heads * dv), BF16), compiler_params=_cparams(2),
        name="mla_ctx_attn",
    )(q, kv, kpe, kv)
    return jnp.concatenate([out, out_c], axis=0)


def _sw_kernel(*refs, scale, seq, latent):
    if latent:
        sink_ref, q_ref, k_ref, v_ref, kc_ref, vc_ref, o_ref = refs
    else:
        sink_ref, q_ref, kc_ref, vc_ref, o_ref = refs
    kvh = pl.program_id(1)
    bi = pl.program_id(2)
    group = SW_HEADS // SW_KV_HEADS
    npair = group // 2
    lane = lax.broadcasted_iota(jnp.int32, (1, LANE), 1)
    lo = lane < (LANE // 2)
    mine = (lane >> 6) == (kvh & 1)

    def dup(x):
        xf = x.astype(F32)
        return jnp.where(mine, xf, pltpu.roll(xf, LANE // 2, 1)).astype(BF16)

    q = q_ref[...]
    zero = jnp.zeros((), q.dtype)
    pairs = [q[:, p * LANE:(p + 1) * LANE] for p in range(npair)]
    qs = jnp.concatenate([jnp.where(lo, qp, zero) for qp in pairs] + [jnp.where(lo, zero, qp) for qp in pairs],
                         axis=0)
    kcd, vcd = dup(kc_ref[...]), dup(vc_ref[...])
    s_c = _qk(qs, kcd) * scale
    if latent:
        band = SW_QB + 2 * SW_WINDOW
        s0 = pl.multiple_of(jnp.clip(bi * SW_QB - SW_WINDOW, 0, seq - band), SW_QB)
        kb, vb = dup(k_ref[pl.ds(s0, band), :]), dup(v_ref[pl.ds(s0, band), :])
        s_w = _qk(qs, kb) * scale
        qpos = bi * SW_QB + lax.broadcasted_iota(jnp.int32, (SW_QB, band), 0)
        kpos = s0 + lax.broadcasted_iota(jnp.int32, (SW_QB, band), 1)
        valid = jnp.abs(kpos - qpos) <= SW_WINDOW
    outs = []
    for r in range(group):
        head = 2 * (r % npair) + r // npair
        sk = sink_ref[kvh * group + head]
        parts = [(s_c[r * SW_QB:(r + 1) * SW_QB], vcd)]
        if latent:
            parts.append((jnp.where(valid, s_w[r * SW_QB:(r + 1) * SW_QB], NEG_INF), vb))
        outs.append(_softmax_pv(parts, extra_logit=sk))
    out = jnp.concatenate([jnp.where(lo, outs[p], outs[npair + p]) for p in range(npair)], axis=1)
    o_ref[...] = out.astype(o_ref.dtype)


def _sw_attention(q, kv, sink, batch, seq, ctx_len, need_ctx):
    n, hd = q.shape
    kvh = SW_KV_HEADS
    gw = hd // kvh
    dh = hd // SW_HEADS
    scale = dh ** -0.5
    nb = seq // SW_QB
    cb = (batch * seq) // ctx_len
    vo = (kvh * dh) // LANE
    smem = pl.BlockSpec(memory_space=pltpu.SMEM)
    out = pl.pallas_call(
        functools.partial(_sw_kernel, scale=scale, seq=seq, latent=True),
        grid=(batch, kvh, nb),
        in_specs=[smem,
                  pl.BlockSpec((SW_QB, gw), lambda b, g, i: (b * nb + i, g)),
                  pl.BlockSpec((seq, LANE), lambda b, g, i: (b, g // 2)),
                  pl.BlockSpec((seq, LANE), lambda b, g, i: (b, vo + g // 2)),
                  pl.BlockSpec((ctx_len, LANE), lambda b, g, i: (cb + b, g // 2)),
                  pl.BlockSpec((ctx_len, LANE), lambda b, g, i: (cb + b, vo + g // 2))],
        out_specs=pl.BlockSpec((SW_QB, gw), lambda b, g, i: (b * nb + i, g)),
        out_shape=jax.ShapeDtypeStruct((batch * seq, hd), BF16), compiler_params=_cparams(3), name="sw_attn",
    )(sink, q, kv, kv, kv, kv)
    if not need_ctx:
        return out
    ncb = ctx_len // SW_QB
    qcb = (batch * seq) // SW_QB
    out_c = pl.pallas_call(
        functools.partial(_sw_kernel, scale=scale, seq=seq, latent=False),
        grid=(batch, kvh, ncb),
        in_specs=[smem,
                  pl.BlockSpec((SW_QB, gw), lambda b, g, i: (qcb + b * ncb + i, g)),
                  pl.BlockSpec((ctx_len, LANE), lambda b, g, i: (cb + b, g // 2)),
                  pl.BlockSpec((ctx_len, LANE), lambda b, g, i: (cb + b, vo + g // 2))],
        out_specs=pl.BlockSpec((SW_QB, gw), lambda b, g, i: (b * ncb + i, g)),
        out_shape=jax.ShapeDtypeStruct((batch * ctx_len, hd), BF16), compiler_params=_cparams(3),
        name="sw_ctx_attn",
    )(sink, q, kv, kv)
    return jnp.concatenate([out, out_c], axis=0)


def _gather_rows(src_hbm, idx_ref, base, dst, sem, rows):
    def issue(r, carry):
        pltpu.make_async_copy(src_hbm.at[pl.ds(idx_ref[base + r], 1)], dst.at[pl.ds(r, 1)], sem).start()
        return carry
    lax.fori_loop(0, rows, issue, 0)
    pltpu.make_async_copy(src_hbm.at[pl.ds(0, rows)], dst, sem).wait()


def _moe_gather_kernel(idx_ref, h_hbm, g_hbm, a_ref, gs_ref, hbuf, gbuf, sems):
    base = pl.program_id(0) * TR
    _gather_rows(h_hbm, idx_ref, base, hbuf, sems.at[0], TR)
    _gather_rows(g_hbm, idx_ref, base, gbuf, sems.at[1], TR)
    a_ref[...] = hbuf[...].astype(a_ref.dtype)
    gs_ref[...] = gbuf[...]


def _moe_gather(h, gts, src_idx):
    d = h.shape[1]
    p = src_idx.shape[0]
    grid_spec = pltpu.PrefetchScalarGridSpec(
        num_scalar_prefetch=1, grid=(p // TR,),
        in_specs=[pl.BlockSpec(memory_space=pl.ANY), pl.BlockSpec(memory_space=pl.ANY)],
        out_specs=[pl.BlockSpec((TR, d), lambda t, idx: (t, 0)), pl.BlockSpec((TR, LANE), lambda t, idx: (t, 0))],
        scratch_shapes=[pltpu.VMEM((TR, d), F32), pltpu.VMEM((TR, LANE), F32), pltpu.SemaphoreType.DMA((2,))])
    return pl.pallas_call(
        _moe_gather_kernel, grid_spec=grid_spec,
        out_shape=[jax.ShapeDtypeStruct((p, d), BF16), jax.ShapeDtypeStruct((p, LANE), F32)],
        compiler_params=_cparams(1), name="moe_gather",
    )(src_idx, h, gts)


def _moe_up_kernel(tg_ref, chg_ref, nu_ref, a_ref, gs_ref, wg_ref, wu_ref, o_ref, wgs, wus, *, halves):
    j = pl.program_id(0)
    t = pl.program_id(1)

    @pl.when(chg_ref[t] == 1)
    def _():
        wgs[...] = wg_ref[...].astype(BF16)
        wus[...] = wu_ref[...].astype(BF16)

    @pl.when(t < nu_ref[0])
    def _():
        a = a_ref[...]
        g = jnp.dot(a, wgs[...], preferred_element_type=F32)
        u = jnp.dot(a, wus[...], preferred_element_type=F32)
        expert = tg_ref[t] * EXPERTS_PER_GROUP + j // halves
        lane = lax.broadcasted_iota(jnp.int32, gs_ref.shape, 1)
        gate = jnp.sum(jnp.where(lane == expert, gs_ref[...], 0.0), axis=-1, keepdims=True)
        o_ref[...] = (g * jax.nn.sigmoid(g) * u * gate).astype(o_ref.dtype)

    @pl.when(t >= nu_ref[0])
    def _():
        o_ref[...] = jnp.zeros_like(o_ref)


def _moe_up(a_sorted, g_sorted, wg, wu, layer, tile_group, changed, n_used):
    p, d = a_sorted.shape
    de = wg.shape[-1]
    tn = min(de, 256)
    halves = de // tn
    nt = p // TM
    w_spec = pl.BlockSpec((None, None, d, tn),
                          lambda j, t, tg, ch, nu: (layer, tg[t] * EXPERTS_PER_GROUP + j // halves, 0, j % halves))
    grid_spec = pltpu.PrefetchScalarGridSpec(
        num_scalar_prefetch=3, grid=(EXPERTS_PER_GROUP * halves, nt),
        in_specs=[pl.BlockSpec((TM, d), lambda j, t, tg, ch, nu: (t, 0)),
                  pl.BlockSpec((TM, LANE), lambda j, t, tg, ch, nu: (t, 0)),
                  w_spec, w_spec],
        out_specs=pl.BlockSpec((TM, tn), lambda j, t, tg, ch, nu: (t, j)),
        scratch_shapes=[pltpu.VMEM((d, tn), BF16), pltpu.VMEM((d, tn), BF16)])
    return pl.pallas_call(
        functools.partial(_moe_up_kernel, halves=halves), grid_spec=grid_spec,
        out_shape=jax.ShapeDtypeStruct((p, EXPERTS_PER_GROUP * de), BF16),
        compiler_params=_cparams(2), name="moe_up",
    )(tile_group, changed, n_used, a_sorted, g_sorted, wg, wu)


def _moe_down_kernel(tg_ref, chg_ref, nu_ref, h_ref, w_ref, o_ref, ws):
    t = pl.program_id(1)

    @pl.when(chg_ref[t] == 1)
    def _():
        ws[...] = w_ref[...].astype(BF16)

    @pl.when(t < nu_ref[0])
    def _():
        o_ref[...] = jnp.dot(h_ref[...], ws[...], preferred_element_type=F32)

    @pl.when(t >= nu_ref[0])
    def _():
        o_ref[...] = jnp.zeros_like(o_ref)


def _moe_down(hid, wd4, layer, tile_group, changed, n_used):
    p, kh = hid.shape
    d = wd4.shape[-1]
    tn = min(d, 512)
    grid_spec = pltpu.PrefetchScalarGridSpec(
        num_scalar_prefetch=3, grid=(d // tn, p // TM),
        in_specs=[pl.BlockSpec((TM, kh), lambda j, t, tg, ch, nu: (t, 0)),
                  pl.BlockSpec((None, None, kh, tn), lambda j, t, tg, ch, nu: (layer, tg[t], 0, j))],
        out_specs=pl.BlockSpec((TM, tn), lambda j, t, tg, ch, nu: (t, j)),
        scratch_shapes=[pltpu.VMEM((kh, tn), BF16)])
    return pl.pallas_call(
        _moe_down_kernel, grid_spec=grid_spec, out_shape=jax.ShapeDtypeStruct((p, d), F32),
        compiler_params=_cparams(2), name="moe_down",
    )(tile_group, changed, n_used, hid, wd4)


def _moe_combine_kernel(pos_ref, y_hbm, x_ref, gate_ref, o_ref, ybuf, sem):
    _gather_rows(y_hbm, pos_ref, pl.program_id(0) * TR, ybuf, sem.at[0], TR)
    o_ref[...] = x_ref[...] + gate_ref[...] * ybuf[...]


def _moe_combine(y_sorted, pos, x, mods, layer, n_rows, n_lat, seq):
    d = x.shape[1]
    midx = functools.partial(_mod_index, tile=TR, n_lat=n_lat, seq=seq)
    grid_spec = pltpu.PrefetchScalarGridSpec(
        num_scalar_prefetch=1, grid=(n_rows // TR,),
        in_specs=[pl.BlockSpec(memory_space=pl.ANY),
                  pl.BlockSpec((TR, d), lambda i, pos: (i, 0)),
                  pl.BlockSpec((None, None, None, 1, d), lambda i, pos: (layer, midx(i), 5, 0, 0))],
        out_specs=pl.BlockSpec((TR, d), lambda i, pos: (i, 0)),
        scratch_shapes=[pltpu.VMEM((TR, d), F32), pltpu.SemaphoreType.DMA((1,))])
    return pl.pallas_call(
        _moe_combine_kernel, grid_spec=grid_spec, out_shape=jax.ShapeDtypeStruct((n_rows, d), F32),
        compiler_params=_cparams(1), name="moe_combine",
    )(pos, y_sorted, x, mods)


def _moe_plan(gid, n_rows):
    onehot = (gid[:, None] == jnp.arange(N_GROUPS, dtype=jnp.int32)[None, :]).astype(jnp.int32)
    csum = jnp.cumsum(onehot, axis=0)
    counts = csum[-1]
    rank = jnp.sum(csum * onehot, axis=1) - 1
    tiles = (counts + TM - 1) // TM
    tile_end = jnp.cumsum(tiles)
    pad_off = (tile_end - tiles) * TM
    pos = jnp.sum(onehot * pad_off[None, :], axis=1) + rank
    nt = n_rows // TM + N_GROUPS
    src = jnp.zeros((nt * TM,), jnp.int32).at[pos].set(jnp.arange(n_rows, dtype=jnp.int32))
    t_ids = jnp.arange(nt, dtype=jnp.int32)
    tile_group = jnp.minimum(jnp.sum((t_ids[:, None] >= tile_end[None, :]).astype(jnp.int32), axis=1), N_GROUPS - 1)
    changed = jnp.concatenate([jnp.ones((1,), jnp.int32),
                               (tile_group[1:] != tile_group[:-1]).astype(jnp.int32)])
    return pos.astype(jnp.int32), src, tile_group.astype(jnp.int32), changed, tile_end[-1:].astype(jnp.int32)


def _moe_layer(x, mods, layer, g2, wr2, rb, wg, wu, wd4, n_rows, n_lat, seq):
    h, gts = _norm_route(x, g2, mods, layer, wr2, rb, n_rows, n_lat, seq)
    gid = gts[:, N_EXPERTS].astype(jnp.int32)
    pos, src, tile_group, changed, n_used = _moe_plan(gid, n_rows)
    a_sorted, g_sorted = _moe_gather(h, gts, src)
    hid = _moe_up(a_sorted, g_sorted, wg, wu, layer, tile_group, changed, n_used)
    y_sorted = _moe_down(hid, wd4, layer, tile_group, changed, n_used)
    return _moe_combine(y_sorted, pos, x, mods, layer, n_rows, n_lat, seq)


def _rope_tables(batch, seq, n_rows, rot_dim):
    t = jnp.arange(seq)
    row = (t // GRID_W).astype(F32)
    col = (t % GRID_W).astype(F32)
    d_ax = rot_dim // 2
    inv = ROPE_THETA ** (-jnp.arange(0, d_ax, 2, dtype=F32) / d_ax)
    ar, ac = row[:, None] * inv, col[:, None] * inv
    cos = jnp.concatenate([jnp.cos(ar), jnp.cos(ar), jnp.cos(ac), jnp.cos(ac)], axis=1)
    sin = jnp.concatenate([-jnp.sin(ar), jnp.sin(ar), -jnp.sin(ac), jnp.sin(ac)], axis=1)
    reps = LANE // cos.shape[1]
    cos = jnp.tile(cos, (batch, reps))
    sin = jnp.tile(sin, (batch, reps))
    n_ctx = n_rows - batch * seq
    cos = jnp.concatenate([cos, jnp.ones((n_ctx, LANE), F32)], axis=0)
    sin = jnp.concatenate([sin, jnp.zeros((n_ctx, LANE), F32)], axis=0)
    return cos, sin


def kernel(x, c, ctx, c_ctx, norm1_g, norm2_g, final_g, mod_down, mod_up, mod_b, na_wqkv, na_wo, na_rpb,
           mla_wdq, mla_qnorm, mla_wuq, mla_wdkv, mla_kvnorm, mla_wukv, mla_wo, sw_wq, sw_wkv, sw_wo, sw_sink,
           router_w, router_b, moe_wg, moe_wu, moe_wd):
    batch, seq, d = x.shape
    ctx_len = ctx.shape[1]
    depth = norm1_g.shape[0]
    n_lat = batch * seq
    n_all = n_lat + batch * ctx_len
    assert seq % TM == 0 and (batch * ctx_len) % TM == 0 and ctx_len % SW_QB == 0
    assert seq % (NA_ROW_BLOCK * GRID_W) == 0 and seq // GRID_W >= NA_UNION and 1 + batch <= 8

    xs = jnp.concatenate([x.reshape(n_lat, d), ctx.reshape(batch * ctx_len, d)], axis=0)

    cond8 = jnp.zeros((8, d), F32).at[0].set(c_ctx).at[1:1 + batch].set(c)
    mods = _adaln_all(cond8, mod_down, mod_up, mod_b)[:, :1 + batch].reshape(depth, 1 + batch, 6, 1, d)

    cos, sin = _rope_tables(batch, seq, n_all, MLA_ROPE_DIM)
    rope_extra = (cos, sin)

    w_hi = router_w.astype(BF16)
    w_lo = (router_w - w_hi.astype(F32)).astype(BF16)
    wr2 = jnp.concatenate([w_hi, w_lo, jnp.zeros((d, LANE - 2 * N_EXPERTS), BF16)], axis=1)
    rb = jnp.concatenate([router_b.astype(F32), jnp.zeros((LANE - N_EXPERTS,), F32)]).reshape(1, LANE)

    g1 = norm1_g.reshape(depth, 1, d)
    g2 = norm2_g.reshape(depth, 1, d)
    de = moe_wd.shape[2]
    wd4 = moe_wd.reshape(depth, N_GROUPS, EXPERTS_PER_GROUP * de, d)

    na_patterns, na_tab = _na_tables(seq // GRID_W)
    na_i = mla_i = sw_i = 0
    for layer in range(depth):
        need_ctx = layer < depth - 1
        n_out = n_all if need_ctx else n_lat
        h = _norm_mod(xs, g1, mods, layer, 0, n_all, n_lat, seq)
        kind = layer % N_MIXERS
        if kind == 0:
            qkv = _mm(h, na_wqkv, layer=na_i, n_rows=n_all, tn=512, tk=d)
            slabs = _na_bias_slabs(na_rpb[na_i], na_patterns)
            o = _na_attention(qkv, slabs, jnp.asarray(na_tab), batch, seq, ctx_len, need_ctx)
            w_o, w_o_i = na_wo, na_i
            na_i += 1
        elif kind == 1:
            heads = MLA_HEADS
            qr = mla_wdq.shape[2]
            kvr = mla_kvnorm.shape[1]
            cq = _mm(h, mla_wdq, layer=mla_i, n_rows=n_all, tn=qr, tk=min(d, 2048), epi="rms",
                     rms_width=qr, extra=(mla_qnorm[mla_i].reshape(1, qr),),
                     extra_specs=[pl.BlockSpec((1, qr), lambda j, i, k: (0, 0))])
            wuq = mla_wuq[mla_i].reshape(qr, heads, MLA_NOPE_DIM + MLA_ROPE_DIM)
            wuq = jnp.pad(wuq, ((0, 0), (0, 0), (0, 2 * LANE - MLA_NOPE_DIM - MLA_ROPE_DIM))).reshape(qr, heads * 2 * LANE)
            tnq = min(1024, heads * 2 * LANE)
            q = _mm(cq, wuq, n_rows=n_all, tn=tnq, tk=qr, epi="rope", extra=rope_extra,
                    extra_specs=_rope_specs(), rope_mask=[gi % 2 == 1 for gi in range(tnq // LANE)])
            wdkv = jnp.pad(mla_wdkv[mla_i], ((0, 0), (0, LANE - MLA_ROPE_DIM)))
            ckv, kpe = _mm(h, wdkv, n_rows=n_all, tn=kvr + LANE, tk=d, epi="rms", rms_width=kvr,
                           extra=(mla_kvnorm[mla_i].reshape(1, kvr), cos, sin),
                           extra_specs=[pl.BlockSpec((1, kvr), lambda j, i, k: (0, 0)), *_rope_specs()],
                           out_widths=(kvr, LANE))
            wukv = mla_wukv[mla_i].reshape(kvr, heads, MLA_NOPE_DIM + MLA_V_DIM)
            wukv = jnp.concatenate([wukv[:, :, :MLA_NOPE_DIM].reshape(kvr, heads * MLA_NOPE_DIM),
                                    wukv[:, :, MLA_NOPE_DIM:].reshape(kvr, heads * MLA_V_DIM)], axis=1)
            kv = _mm(ckv, wukv, n_rows=n_all, tn=min(1024, wukv.shape[1]), tk=kvr)
            o = _mla_attention(q, kv, kpe, batch, seq, ctx_len, need_ctx)
            w_o, w_o_i = mla_wo, mla_i
            mla_i += 1
        else:
            hd = sw_wq.shape[2]
            kvw = sw_wkv.shape[2]
            q = _mm(h, sw_wq, layer=sw_i, n_rows=n_all, tn=512, tk=d, epi="rope", extra=rope_extra,
                    extra_specs=_rope_specs(), rope_mask=[True] * (512 // LANE))
            kv = _mm(h, sw_wkv, layer=sw_i, n_rows=n_all, tn=kvw, tk=min(d, 2048), epi="rope", extra=rope_extra,
                     extra_specs=_rope_specs(), rope_mask=[gi < kvw // (2 * LANE) for gi in range(kvw // LANE)])
            o = _sw_attention(q, kv, sw_sink[sw_i].astype(F32), batch, seq, ctx_len, need_ctx)
            w_o, w_o_i = sw_wo, sw_i
            sw_i += 1
        xs = _mm_resid(o, w_o, w_o_i, xs, mods, layer, 2, n_out, n_lat, seq)
        xs = _moe_layer(xs, mods, layer, g2, wr2, rb, moe_wg, moe_wu, wd4, n_out, n_lat, seq)
    return _final_norm(xs, final_g.reshape(1, d), n_lat).reshape(batch, seq, d)
```
